```python
import jax, jax.numpy as jnp
from jax import lax
import numpy as np

D_MODEL = 4096
BATCH = 2
SEQ = 8192
DEPTH = 1

CHUNK = 64
HEAD_DIM = 128
D_MIX = D_MODEL
N_HEADS_A = D_MIX // (2 * HEAD_DIM)
N_HEADS_B = D_MIX // (2 * HEAD_DIM)
D_A = N_HEADS_A * HEAD_DIM
D_B = N_HEADS_B * HEAD_DIM
D_IN = 3 * (D_A + D_B)
LEFT_CHUNKS = 8
BAND = LEFT_CHUNKS + 1
MAX_REL = 2 * CHUNK
QBLK = 128
N_GROUPS = 4
EXPERTS_PER_GROUP = 8
N_EXPERTS = N_GROUPS * EXPERTS_PER_GROUP
TOP_K = 2
D_EXPERT = D_MODEL // 4
EXPERT_BLOCK = 128
EPS = 1e-6
NEG_INF = -1e30

kernel_name = "hybrid_chunked_stickbreak_hmoe_block"


def rms_norm(x, g):
    xf = x.astype(jnp.float32)
    y = xf * lax.rsqrt(jnp.mean(xf * xf, axis=-1, keepdims=True) + EPS)
    return (y * g.astype(jnp.float32)).astype(x.dtype)


def modulate(h, shift, scale):
    return h * (1 + scale[:, None, :]) + shift[:, None, :]


def split_heads(t, n_heads):
    b, s, _ = t.shape
    return t.reshape(b, s, n_heads, HEAD_DIM)


def chunked_relpos_attention(q, k, v, rel_bias):
    b, s, h, dh = q.shape
    nc = s // CHUNK
    qc = q.reshape(b, nc, CHUNK, h, dh)
    pad = ((0, 0), (LEFT_CHUNKS, 0), (0, 0), (0, 0), (0, 0))
    kp = jnp.pad(k.reshape(b, nc, CHUNK, h, dh), pad)
    vp = jnp.pad(v.reshape(b, nc, CHUNK, h, dh), pad)
    scores = jnp.concatenate(
        [jnp.einsum("bnqhd,bnkhd->bnhqk", qc, kp[:, j:j + nc],
                    preferred_element_type=jnp.float32) for j in range(BAND)],
        axis=-1) * (dh ** -0.5)
    qpos = LEFT_CHUNKS * CHUNK + np.arange(CHUNK)
    kpos = np.arange(BAND * CHUNK)
    rel_idx = np.clip(qpos[:, None] - kpos[None, :], -MAX_REL, MAX_REL) + MAX_REL
    bias = rel_bias.astype(jnp.float32)[:, rel_idx]
    slot_chunk = jnp.arange(nc)[:, None] - LEFT_CHUNKS + jnp.arange(BAND)[None, :]
    valid = jnp.repeat(slot_chunk >= 0, CHUNK, axis=1)
    scores = jnp.where(valid[None, :, None, None, :], scores + bias[None, None], NEG_INF)
    p = jax.nn.softmax(scores, axis=-1).astype(v.dtype).reshape(b, nc, h, CHUNK, BAND, CHUNK)
    out = jnp.zeros((b, nc, CHUNK, h, dh), jnp.float32)
    for j in range(BAND):
        out = out + jnp.einsum("bnhqk,bnkhd->bnqhd", p[:, :, :, :, j], vp[:, j:j + nc],
                               preferred_element_type=jnp.float32)
    return out.astype(v.dtype).reshape(b, s, h, dh)


def stick_breaking_attention(q, k, v):
    b, s, h, dh = q.shape
    nq = s // QBLK
    qb = jnp.swapaxes(q.reshape(b, nq, QBLK, h, dh), 0, 1)
    kpos = jnp.arange(s)
    scale = dh ** -0.5

    def one_block(args):
        q_blk, blk = args
        z = jnp.einsum("bqhd,bkhd->bhqk", q_blk, k,
                       preferred_element_type=jnp.float32) * scale
        qpos = blk * QBLK + jnp.arange(QBLK)
        causal = (kpos[None, :] < qpos[:, None])[None, None]
        log_beta = jax.nn.log_sigmoid(z)
        log_keep = jnp.where(causal, log_beta - z, 0.0)
        after = lax.cumsum(log_keep, axis=3, reverse=True) - log_keep
        a = jnp.where(causal, jnp.exp(log_beta + after), 0.0)
        return jnp.einsum("bhqk,bkhd->bqhd", a.astype(v.dtype), v,
                          preferred_element_type=jnp.float32).astype(v.dtype)

    o = lax.map(one_block, (qb, jnp.arange(nq)))
    return jnp.swapaxes(o, 0, 1).reshape(b, s, h, dh)


def hierarchical_moe(h, w_group, b_group, w_router, b_router, w_gate, w_up, w_down):
    t, d = h.shape
    g_logits = jnp.matmul(h, w_group, preferred_element_type=jnp.float32) + b_group.astype(jnp.float32)
    g_prob = jax.nn.softmax(g_logits, axis=-1)
    g_sel = jnp.argmax(g_logits, axis=-1)
    p_group = jnp.take_along_axis(g_prob, g_sel[:, None], axis=-1)
    e_logits = (jnp.matmul(h, w_router, preferred_element_type=jnp.float32)
                + b_router.astype(jnp.float32)).reshape(t, N_GROUPS, EXPERTS_PER_GROUP)
    e_logits = jnp.take_along_axis(e_logits, g_sel[:, None, None], axis=1)[:, 0]
    top_val, top_idx = lax.top_k(e_logits, TOP_K)
    gate = jax.nn.softmax(top_val, axis=-1) * p_group
    expert_id = (g_sel[:, None] * EXPERTS_PER_GROUP + top_idx).reshape(-1).astype(jnp.int32)
    token_id = jnp.repeat(jnp.arange(t, dtype=jnp.int32), TOP_K)
    weight = gate.reshape(-1)
    n_assign = t * TOP_K
    order = jnp.argsort(expert_id)
    e_sorted = expert_id[order]
    tok_sorted = token_id[order]
    w_sorted = weight[order]
    counts = jnp.zeros((N_EXPERTS,), jnp.int32).at[expert_id].add(1)
    padded = (counts + EXPERT_BLOCK - 1) // EXPERT_BLOCK * EXPERT_BLOCK
    pad_end = jnp.cumsum(padded)
    pad_start = pad_end - padded
    start = jnp.cumsum(counts) - counts
    dest = pad_start[e_sorted] + jnp.arange(n_assign, dtype=jnp.int32) - start[e_sorted]
    n_slots = n_assign + N_EXPERTS * EXPERT_BLOCK
    n_blocks = n_slots // EXPERT_BLOCK
    slot_tok = jnp.zeros((n_slots,), jnp.int32).at[dest].set(tok_sorted)
    slot_w = jnp.zeros((n_slots,), jnp.float32).at[dest].set(w_sorted)
    block_expert = jnp.minimum(
        jnp.searchsorted(pad_end, jnp.arange(n_blocks, dtype=jnp.int32) * EXPERT_BLOCK, side="right"),
        N_EXPERTS - 1)

    def run_block(args):
        e, tok, wt = args
        xb = h[tok]
        y = (jax.nn.silu(xb @ w_gate[e]) * (xb @ w_up[e])) @ w_down[e]
        return y * wt[:, None].astype(y.dtype)

    yb = lax.map(run_block, (block_expert,
                             slot_tok.reshape(n_blocks, EXPERT_BLOCK),
                             slot_w.reshape(n_blocks, EXPERT_BLOCK)))
    return jnp.zeros_like(h).at[slot_tok].add(yb.reshape(n_slots, d))


def setup_inputs(seed: int = 0) -> dict:
    key = jax.random.key(seed)
    ks = jax.random.split(key, 24)
    f32 = jnp.float32
    nrm = lambda k, shape, s: (jax.random.normal(k, shape, f32) * s)
    gain = lambda k, n: 1.0 + 0.02 * jax.random.normal(k, (DEPTH, n), f32)
    return {
        "x": nrm(ks[0], (BATCH, SEQ, D_MODEL), 1.0),
        "c": nrm(ks[1], (BATCH, D_MODEL), 1.0),
        "w_ada": nrm(ks[2], (DEPTH, D_MODEL, 6 * D_MODEL), 0.5 * D_MODEL ** -0.5),
        "b_ada": nrm(ks[3], (DEPTH, 6 * D_MODEL), 0.02),
        "g_pre_mix": gain(ks[4], D_MODEL),
        "g_post_mix": gain(ks[5], D_MODEL),
        "g_pre_ffn": gain(ks[6], D_MODEL),
        "g_post_ffn": gain(ks[7], D_MODEL),
        "w_in": nrm(ks[8], (DEPTH, D_MODEL, D_IN), D_MODEL ** -0.5),
        "rel_bias": nrm(ks[9], (DEPTH, N_HEADS_A, 2 * MAX_REL + 1), 0.5),
        "g_out_a": gain(ks[10], D_A),
        "g_out_b": gain(ks[11], D_B),
        "w_out": nrm(ks[12], (DEPTH, D_MIX, D_MODEL), D_MIX ** -0.5),
        "w_group": nrm(ks[13], (DEPTH, D_MODEL, N_GROUPS), D_MODEL ** -0.5),
        "b_group": nrm(ks[14], (DEPTH, N_GROUPS), 0.01),
        "w_router": nrm(ks[15], (DEPTH, D_MODEL, N_EXPERTS), D_MODEL ** -0.5),
        "b_router": nrm(ks[16], (DEPTH, N_EXPERTS), 0.01),
        "w_gate": nrm(ks[17], (DEPTH, N_EXPERTS, D_MODEL, D_EXPERT), D_MODEL ** -0.5),
        "w_up": nrm(ks[18], (DEPTH, N_EXPERTS, D_MODEL, D_EXPERT), D_MODEL ** -0.5),
        "w_down": nrm(ks[19], (DEPTH, N_EXPERTS, D_EXPERT, D_MODEL), D_EXPERT ** -0.5),
    }


def reference(x, c, w_ada, b_ada, g_pre_mix, g_post_mix, g_pre_ffn, g_post_ffn, w_in,
              rel_bias, g_out_a, g_out_b, w_out, w_group, b_group, w_router, b_router,
              w_gate, w_up, w_down):
    b, s, d = x.shape
    cut = [D_A, 2 * D_A, 3 * D_A, 3 * D_A + D_B, 3 * D_A + 2 * D_B]
    for layer in range(DEPTH):
        mod = jax.nn.silu(c) @ w_ada[layer] + b_ada[layer]
        shift_m, scale_m, gate_m, shift_f, scale_f, gate_f = jnp.split(mod, 6, axis=-1)
        h = modulate(rms_norm(x, g_pre_mix[layer]), shift_m, scale_m)
        proj = h @ w_in[layer]
        qa, ka, va, qb, kb, vb = jnp.split(proj, cut, axis=-1)
        oa = chunked_relpos_attention(split_heads(qa, N_HEADS_A), split_heads(ka, N_HEADS_A),
                                      split_heads(va, N_HEADS_A), rel_bias[layer])
        ob = stick_breaking_attention(split_heads(qb, N_HEADS_B), split_heads(kb, N_HEADS_B),
                                      split_heads(vb, N_HEADS_B))
        oa = rms_norm(oa, g_out_a[layer].reshape(N_HEADS_A, HEAD_DIM)).reshape(b, s, D_A)
        ob = rms_norm(ob, g_out_b[layer].reshape(N_HEADS_B, HEAD_DIM)).reshape(b, s, D_B)
        y = jnp.concatenate([oa, ob], axis=-1) @ w_out[layer]
        x = x + gate_m[:, None, :] * rms_norm(y, g_post_mix[layer])
        h = modulate(rms_norm(x, g_pre_ffn[layer]), shift_f, scale_f)
        y = hierarchical_moe(h.reshape(b * s, d), w_group[layer], b_group[layer], w_router[layer],
                             b_router[layer], w_gate[layer], w_up[layer], w_down[layer])
        x = x + gate_f[:, None, :] * rms_norm(y.reshape(b, s, d), g_post_ffn[layer])
    return x
```

```python
import functools

import numpy as np
import jax
import jax.numpy as jnp
from jax import lax
from jax.experimental import pallas as pl
from jax.experimental.pallas import tpu as pltpu

F32 = jnp.float32
BF16 = jnp.bfloat16

HEAD_DIM = 128
CHUNK = 64
LEFT_CHUNKS = 8
EPS = 1e-6
NEG_INF = -1e30
LANES = 128
VMEM_LIMIT_BYTES = 56 * 1024 * 1024

ATT_A_BLOCK = LEFT_CHUNKS * CHUNK
SB_BLOCK = 256
MOE_BLOCK = 512
MOE_HIDDEN_TILE = 128


def _cparams(sem):
    return pltpu.CompilerParams(dimension_semantics=sem, vmem_limit_bytes=VMEM_LIMIT_BYTES)


def _tile(n, pref):
    t = min(n, pref)
    while n % t:
        t //= 2
    return t


def _adaln_kernel(c_ref, w_ref, b_ref, o_ref):
    c = c_ref[...]
    a = c * jax.nn.sigmoid(c)
    o_ref[...] = jnp.dot(a.astype(BF16), w_ref[...].astype(BF16),
                         preferred_element_type=F32) + b_ref[...]


def _adaln(c_pad, w_ada, b_ada):
    d, n = w_ada.shape
    tn = _tile(n, 512)
    return pl.pallas_call(
        _adaln_kernel,
        out_shape=jax.ShapeDtypeStruct((c_pad.shape[0], n), F32),
        grid=(n // tn,),
        in_specs=[pl.BlockSpec((c_pad.shape[0], d), lambda j: (0, 0)),
                  pl.BlockSpec((d, tn), lambda j: (0, j)),
                  pl.BlockSpec((1, tn), lambda j: (0, j))],
        out_specs=pl.BlockSpec((c_pad.shape[0], tn), lambda j: (0, j)),
        compiler_params=_cparams(("arbitrary",)),
        name="adaln",
    )(c_pad, w_ada, b_ada.reshape(1, n))


def _prenorm_kernel(x_ref, g_ref, sh_ref, sc_ref, o_ref):
    x = x_ref[...]
    y = x * lax.rsqrt(jnp.mean(x * x, axis=-1, keepdims=True) + EPS) * g_ref[...]
    o_ref[...] = (y * (1.0 + sc_ref[0]) + sh_ref[0]).astype(o_ref.dtype)


def _prenorm(x2, g, mod3, seq, shift_idx, scale_idx):
    t, d = x2.shape
    tm = _tile(seq, 256)
    per_b = seq // tm
    return pl.pallas_call(
        _prenorm_kernel,
        out_shape=jax.ShapeDtypeStruct((t, d), BF16),
        grid=(t // tm,),
        in_specs=[pl.BlockSpec((tm, d), lambda i: (i, 0)),
                  pl.BlockSpec((1, d), lambda i: (0, 0)),
                  pl.BlockSpec((1, 1, d), lambda i: ((i // per_b) * 6 + shift_idx, 0, 0)),
                  pl.BlockSpec((1, 1, d), lambda i: ((i // per_b) * 6 + scale_idx, 0, 0))],
        out_specs=pl.BlockSpec((tm, d), lambda i: (i, 0)),
        compiler_params=_cparams(("arbitrary",)),
        name="prenorm",
    )(x2, g.reshape(1, d), mod3, mod3)


def _proj_kernel(a_ref, b_ref, s_ref, o_ref):
    acc = jnp.dot(a_ref[...], b_ref[...], preferred_element_type=F32)
    o_ref[...] = (acc * s_ref[...]).astype(o_ref.dtype)


def _proj(a, w, colscale):
    m, k = a.shape
    n = w.shape[1]
    tm, tn = _tile(m, 1024), _tile(n, 1024)
    return pl.pallas_call(
        _proj_kernel,
        out_shape=jax.ShapeDtypeStruct((m, n), BF16),
        grid=(m // tm, n // tn),
        in_specs=[pl.BlockSpec((tm, k), lambda i, j: (i, 0)),
                  pl.BlockSpec((k, tn), lambda i, j: (0, j)),
                  pl.BlockSpec((1, tn), lambda i, j: (0, j))],
        out_specs=pl.BlockSpec((tm, tn), lambda i, j: (i, j)),
        compiler_params=_cparams(("arbitrary", "arbitrary")),
        name="qkv_proj",
    )(a, w, colscale)


def _outproj_kernel(a1_ref, a2_ref, w1_ref, w2_ref, o_ref):
    o_ref[...] = (jnp.dot(a1_ref[...], w1_ref[...], preferred_element_type=F32)
                  + jnp.dot(a2_ref[...], w2_ref[...], preferred_element_type=F32))


def _outproj(oa, ob, w_out):
    m, ka = oa.shape
    kb = ob.shape[1]
    assert ka == kb
    n = w_out.shape[1]
    tm, tn = _tile(m, 1024), _tile(n, 1024)
    return pl.pallas_call(
        _outproj_kernel,
        out_shape=jax.ShapeDtypeStruct((m, n), F32),
        grid=(m // tm, n // tn),
        in_specs=[pl.BlockSpec((tm, ka), lambda i, j: (i, 0)),
                  pl.BlockSpec((tm, kb), lambda i, j: (i, 0)),
                  pl.BlockSpec((ka, tn), lambda i, j: (0, j)),
                  pl.BlockSpec((kb, tn), lambda i, j: (1, j))],
        out_specs=pl.BlockSpec((tm, tn), lambda i, j: (i, j)),
        compiler_params=_cparams(("arbitrary", "arbitrary")),
        name="out_proj",
    )(oa, ob, w_out, w_out)


def _head_rms(o, g):
    return o * lax.rsqrt(jnp.mean(o * o, axis=-1, keepdims=True) + EPS) * g


def _attn_a_kernel(q_ref, kp_ref, kc_ref, vp_ref, vc_ref, bp_ref, bc_ref, g_ref, o_ref):
    i = pl.program_id(2)
    q = q_ref[...]
    dn = (((1,), (1,)), ((), ()))
    sp = lax.dot_general(q, kp_ref[...], dn, preferred_element_type=F32) + bp_ref[0]
    sp = jnp.where(i > 0, sp, NEG_INF)
    sc = lax.dot_general(q, kc_ref[...], dn, preferred_element_type=F32) + bc_ref[0]
    m = jnp.maximum(jnp.max(sp, axis=-1, keepdims=True), jnp.max(sc, axis=-1, keepdims=True))
    pp = jnp.exp(sp - m)
    pc = jnp.exp(sc - m)
    denom = jnp.sum(pp, axis=-1, keepdims=True) + jnp.sum(pc, axis=-1, keepdims=True)
    o = (jnp.dot(pp.astype(BF16), vp_ref[...], preferred_element_type=F32)
         + jnp.dot(pc.astype(BF16), vc_ref[...], preferred_element_type=F32))
    o = o / denom
    o_ref[...] = _head_rms(o, g_ref[0]).astype(o_ref.dtype)


def _attn_a_bias(rel_bias):
    max_rel = (rel_bias.shape[-1] - 1) // 2
    tq = ATT_A_BLOCK
    a = np.arange(tq)[:, None]
    b = np.arange(tq)[None, :]
    d_prev = tq + a - b
    d_cur = a - b
    ok_prev = (b // CHUNK) >= (a // CHUNK)
    ok_cur = (b // CHUNK) <= (a // CHUNK)
    rb = rel_bias.astype(F32)
    bp = jnp.where(ok_prev[None], rb[:, np.clip(d_prev, -max_rel, max_rel) + max_rel], NEG_INF)
    bc = jnp.where(ok_cur[None], rb[:, np.clip(d_cur, -max_rel, max_rel) + max_rel], NEG_INF)
    return bp, bc


def _attn_a(qkv, rel_bias, g_out, batch, seq, n_heads):
    tq = ATT_A_BLOCK
    assert seq % tq == 0
    nq = seq // tq
    bp, bc = _attn_a_bias(rel_bias)
    koff, voff = n_heads, 2 * n_heads
    q_spec = pl.BlockSpec((tq, HEAD_DIM), lambda b, h, i: (b * nq + i, h))

    def prev(off):
        return pl.BlockSpec((tq, HEAD_DIM), lambda b, h, i: (b * nq + jnp.maximum(i - 1, 0), off + h))

    def cur(off):
        return pl.BlockSpec((tq, HEAD_DIM), lambda b, h, i: (b * nq + i, off + h))

    bias_spec = pl.BlockSpec((1, tq, tq), lambda b, h, i: (h, 0, 0))
    return pl.pallas_call(
        _attn_a_kernel,
        out_shape=jax.ShapeDtypeStruct((batch * seq, n_heads * HEAD_DIM), BF16),
        grid=(batch, n_heads, nq),
        in_specs=[q_spec, prev(koff), cur(koff), prev(voff), cur(voff), bias_spec, bias_spec,
                  pl.BlockSpec((1, 1, HEAD_DIM), lambda b, h, i: (h, 0, 0))],
        out_specs=pl.BlockSpec((tq, HEAD_DIM), lambda b, h, i: (b * nq + i, h)),
        compiler_params=_cparams(("arbitrary", "arbitrary", "arbitrary")),
        name="attn_chunked",
    )(qkv, qkv, qkv, qkv, qkv, bp, bc, g_out.reshape(n_heads, 1, HEAD_DIM))


def _sb_block(q, k, v, tri, carry, acc, causal):
    z = lax.dot_general(q, k, (((1,), (1,)), ((), ())), preferred_element_type=F32)
    t = jnp.log(1.0 + jnp.exp(-jnp.abs(z)))
    log_beta = jnp.minimum(z, 0.0) - t
    log_keep = log_beta - z
    if causal is not None:
        log_keep = jnp.where(causal, log_keep, 0.0)
    after = jnp.dot(log_keep.astype(BF16), tri, preferred_element_type=F32) + carry
    a = jnp.exp(log_beta + after)
    if causal is not None:
        a = jnp.where(causal, a, 0.0)
    acc = acc + jnp.dot(a.astype(BF16), v, preferred_element_type=F32)
    carry = carry + jnp.sum(log_keep, axis=-1, keepdims=True)
    return carry, acc


def _sb_kernel(q_ref, k_ref, v_ref, g_ref, o_ref, *, blk):
    qi = pl.program_id(2)
    q = q_ref[...]
    row = lax.broadcasted_iota(jnp.int32, (blk, blk), 0)
    col = lax.broadcasted_iota(jnp.int32, (blk, blk), 1)
    tri = (row > col).astype(BF16)
    causal = col < row

    def kv(j):
        start = pl.multiple_of(j * blk, blk)
        return k_ref[pl.ds(start, blk), :], v_ref[pl.ds(start, blk), :]

    k, v = kv(qi)
    carry, acc = _sb_block(q, k, v, tri, jnp.zeros((blk, 1), F32),
                           jnp.zeros((blk, HEAD_DIM), F32), causal)

    def body(it, state):
        k, v = kv(qi - 1 - it)
        return _sb_block(q, k, v, tri, state[0], state[1], None)

    carry, acc = lax.fori_loop(0, qi, body, (carry, acc))
    o_ref[...] = _head_rms(acc, g_ref[0]).astype(o_ref.dtype)


def _attn_sb(qkv, g_out, batch, seq, n_heads, col0):
    blk = _tile(seq, SB_BLOCK)
    nq = seq // blk
    qoff, koff, voff = col0, col0 + n_heads, col0 + 2 * n_heads
    return pl.pallas_call(
        functools.partial(_sb_kernel, blk=blk),
        out_shape=jax.ShapeDtypeStruct((batch * seq, n_heads * HEAD_DIM), BF16),
        grid=(batch, n_heads, nq),
        in_specs=[pl.BlockSpec((blk, HEAD_DIM), lambda b, h, i: (b * nq + i, qoff + h)),
                  pl.BlockSpec((seq, HEAD_DIM), lambda b, h, i: (b, koff + h)),
                  pl.BlockSpec((seq, HEAD_DIM), lambda b, h, i: (b, voff + h)),
                  pl.BlockSpec((1, 1, HEAD_DIM), lambda b, h, i: (h, 0, 0))],
        out_specs=pl.BlockSpec((blk, HEAD_DIM), lambda b, h, i: (b * nq + i, h)),
        compiler_params=_cparams(("arbitrary", "arbitrary", "arbitrary")),
        name="attn_stickbreak",
    )(qkv, qkv, qkv, g_out.reshape(n_heads, 1, HEAD_DIM))


def _split_bf16(x):
    hi = x.astype(BF16)
    lo = (x - hi.astype(F32)).astype(BF16)
    return hi, lo


def _router_kernel(y_ref, x_ref, gpost_ref, gate_ref, gpre_ref, sh_ref, sc_ref, whi_ref, wlo_ref,
                   brt_ref, x1_ref, h2_ref, ids_ref, gates_ref, *, n_groups, per_group):
    y = y_ref[...]
    yn = y * lax.rsqrt(jnp.mean(y * y, axis=-1, keepdims=True) + EPS) * gpost_ref[...]
    x1 = x_ref[...] + gate_ref[0] * yn
    x1_ref[...] = x1
    hn = x1 * lax.rsqrt(jnp.mean(x1 * x1, axis=-1, keepdims=True) + EPS) * gpre_ref[...]
    h2 = hn * (1.0 + sc_ref[0]) + sh_ref[0]
    h2_ref[...] = h2
    hi, lo = _split_bf16(h2)
    whi = whi_ref[...]
    logits = (jnp.dot(hi, whi, preferred_element_type=F32)
              + jnp.dot(lo, whi, preferred_element_type=F32)
              + jnp.dot(hi, wlo_ref[...], preferred_element_type=F32)) + brt_ref[...]
    lane = lax.broadcasted_iota(jnp.int32, logits.shape, 1)
    big = jnp.int32(LANES)
    is_g = lane < n_groups
    gl = jnp.where(is_g, logits, -jnp.inf)
    gmax = jnp.max(gl, axis=-1, keepdims=True)
    gsel = jnp.min(jnp.where(gl == gmax, lane, big), axis=-1, keepdims=True)
    p_group = 1.0 / jnp.sum(jnp.where(is_g, jnp.exp(logits - gmax), 0.0), axis=-1, keepdims=True)
    lo_lane = n_groups + gsel * per_group
    in_grp = (lane >= lo_lane) & (lane < lo_lane + per_group)
    el = jnp.where(in_grp, logits, -jnp.inf)
    v1 = jnp.max(el, axis=-1, keepdims=True)
    i1 = jnp.min(jnp.where(el == v1, lane, big), axis=-1, keepdims=True)
    el2 = jnp.where(lane == i1, -jnp.inf, el)
    v2 = jnp.max(el2, axis=-1, keepdims=True)
    i2 = jnp.min(jnp.where(el2 == v2, lane, big), axis=-1, keepdims=True)
    e21 = jnp.exp(v2 - v1)
    w1 = p_group / (1.0 + e21)
    w2 = p_group * e21 / (1.0 + e21)
    ids_ref[...] = jnp.where(lane == 0, i1 - n_groups, jnp.where(lane == 1, i2 - n_groups, 0))
    gates_ref[...] = jnp.where(lane == 0, w1, jnp.where(lane == 1, w2, 0.0))


def _router(y, x2, g_post, g_pre, mod3, seq, gate_idx, shift_idx, scale_idx, w_rt_hi, w_rt_lo, b_rt,
            n_groups, per_group):
    t, d = x2.shape
    tm = _tile(seq, 256)
    per_b = seq // tm
    row = pl.BlockSpec((tm, d), lambda i: (i, 0))
    vec = pl.BlockSpec((1, d), lambda i: (0, 0))

    def modspec(idx):
        return pl.BlockSpec((1, 1, d), lambda i: ((i // per_b) * 6 + idx, 0, 0))

    wspec = pl.BlockSpec((d, LANES), lambda i: (0, 0))
    lane_out = pl.BlockSpec((tm, LANES), lambda i: (i, 0))
    return pl.pallas_call(
        functools.partial(_router_kernel, n_groups=n_groups, per_group=per_group),
        out_shape=(jax.ShapeDtypeStruct((t, d), F32), jax.ShapeDtypeStruct((t, d), F32),
                   jax.ShapeDtypeStruct((t, LANES), jnp.int32), jax.ShapeDtypeStruct((t, LANES), F32)),
        grid=(t // tm,),
        in_specs=[row, row, vec, modspec(gate_idx), vec, modspec(shift_idx), modspec(scale_idx),
                  wspec, wspec, pl.BlockSpec((1, LANES), lambda i: (0, 0))],
        out_specs=(row, row, lane_out, lane_out),
        compiler_params=_cparams(("arbitrary",)),
        name="residual_router",
    )(y, x2, g_post.reshape(1, d), mod3, g_pre.reshape(1, d), mod3, mod3, w_rt_hi, w_rt_lo, b_rt)


def _row_copy(src_hbm, dst, src_row, dst_row, sem):
    return pltpu.make_async_copy(src_hbm.at[pl.ds(src_row, 1), :], dst.at[pl.ds(dst_row, 1), :], sem)


def _moe_kernel(be_ref, nused_ref, tok_ref, w_ref, h2_hbm, wg_ref, wu_ref, wd_ref, o_ref,
                xs32, xsb, sem, *, rows, n_hidden_tiles):
    i = pl.program_id(0)
    j = pl.program_id(1)
    valid = i < nused_ref[0]

    @pl.when(jnp.logical_and(valid, j == 0))
    def _():
        def issue(r, c):
            _row_copy(h2_hbm, xs32, tok_ref[0, 0, r], r, sem).start()
            return c

        lax.fori_loop(0, rows, issue, 0)

        def wait(r, c):
            _row_copy(h2_hbm, xs32, 0, r, sem).wait()
            return c

        lax.fori_loop(0, rows, wait, 0)
        xsb[...] = xs32[...].astype(BF16)
        o_ref[...] = jnp.zeros_like(o_ref)

    @pl.when(jnp.logical_and(jnp.logical_not(valid), j == 0))
    def _():
        o_ref[...] = jnp.zeros_like(o_ref)

    @pl.when(valid)
    def _():
        x = xsb[...]
        g = jnp.dot(x, wg_ref[0].astype(BF16), preferred_element_type=F32)
        u = jnp.dot(x, wu_ref[0].astype(BF16), preferred_element_type=F32)
        hmid = (g * jax.nn.sigmoid(g)) * u
        o_ref[...] += jnp.dot(hmid.astype(BF16), wd_ref[0].astype(BF16), preferred_element_type=F32)

    @pl.when(jnp.logical_and(valid, j == n_hidden_tiles - 1))
    def _():
        o_ref[...] = o_ref[...] * w_ref[...]


def _moe(h2, w_gate, w_up, w_down, block_expert, n_used, slot_tok, slot_w, rows):
    t, d = h2.shape
    n_exp, _, dh = w_gate.shape
    n_slots = slot_tok.shape[0]
    nblk = n_slots // rows
    th = _tile(dh, MOE_HIDDEN_TILE)
    nh = dh // th

    def hid(i, j, be, nu):
        return jnp.where(i < nu[0], j, nh - 1)

    grid_spec = pltpu.PrefetchScalarGridSpec(
        num_scalar_prefetch=2,
        grid=(nblk, nh),
        in_specs=[pl.BlockSpec((1, 1, rows), lambda i, j, be, nu: (i, 0, 0), memory_space=pltpu.SMEM),
                  pl.BlockSpec((rows, 1), lambda i, j, be, nu: (i, 0)),
                  pl.BlockSpec(memory_space=pl.ANY),
                  pl.BlockSpec((1, d, th), lambda i, j, be, nu: (be[i], 0, hid(i, j, be, nu))),
                  pl.BlockSpec((1, d, th), lambda i, j, be, nu: (be[i], 0, hid(i, j, be, nu))),
                  pl.BlockSpec((1, th, d), lambda i, j, be, nu: (be[i], hid(i, j, be, nu), 0))],
        out_specs=pl.BlockSpec((rows, d), lambda i, j, be, nu: (i, 0)),
        scratch_shapes=[pltpu.VMEM((rows, d), F32), pltpu.VMEM((rows, d), BF16),
                        pltpu.SemaphoreType.DMA(())],
    )
    return pl.pallas_call(
        functools.partial(_moe_kernel, rows=rows, n_hidden_tiles=nh),
        out_shape=jax.ShapeDtypeStruct((n_slots, d), F32),
        grid_spec=grid_spec,
        compiler_params=_cparams(("arbitrary", "arbitrary")),
        name="moe_experts",
    )(block_expert, n_used, slot_tok.reshape(nblk, 1, rows), slot_w.reshape(n_slots, 1), h2,
      w_gate, w_up, w_down)


def _combine_kernel(slots_ref, ys_hbm, x1_ref, g_ref, gate_ref, o_ref, buf, sem, *, tm):
    def issue(r, c):
        _row_copy(ys_hbm, buf.at[0], slots_ref[0, 0, 2 * r], r, sem).start()
        _row_copy(ys_hbm, buf.at[1], slots_ref[0, 0, 2 * r + 1], r, sem).start()
        return c

    lax.fori_loop(0, tm, issue, 0)

    def wait(r, c):
        _row_copy(ys_hbm, buf.at[0], 0, r, sem).wait()
        _row_copy(ys_hbm, buf.at[1], 0, r, sem).wait()
        return c

    lax.fori_loop(0, tm, wait, 0)
    y = buf[0] + buf[1]
    yn = y * lax.rsqrt(jnp.mean(y * y, axis=-1, keepdims=True) + EPS) * g_ref[...]
    o_ref[...] = x1_ref[...] + gate_ref[0] * yn


def _combine(ys, slots, x1, g_post, mod3, seq, gate_idx):
    t, d = x1.shape
    tm = _tile(seq, 256)
    per_b = seq // tm
    nt = t // tm
    return pl.pallas_call(
        functools.partial(_combine_kernel, tm=tm),
        out_shape=jax.ShapeDtypeStruct((t, d), F32),
        grid=(nt,),
        in_specs=[pl.BlockSpec((1, 1, 2 * tm), lambda i: (i, 0, 0), memory_space=pltpu.SMEM),
                  pl.BlockSpec(memory_space=pl.ANY),
                  pl.BlockSpec((tm, d), lambda i: (i, 0)),
                  pl.BlockSpec((1, d), lambda i: (0, 0)),
                  pl.BlockSpec((1, 1, d), lambda i: ((i // per_b) * 6 + gate_idx, 0, 0))],
        out_specs=pl.BlockSpec((tm, d), lambda i: (i, 0)),
        scratch_shapes=[pltpu.VMEM((2, tm, d), F32), pltpu.SemaphoreType.DMA(())],
        compiler_params=_cparams(("arbitrary",)),
        name="moe_combine",
    )(slots.reshape(nt, 1, 2 * tm), ys, x1, g_post.reshape(1, d), mod3)


def _dispatch(expert_id, n_experts, rows):
    n_assign = expert_id.shape[0]
    counts = jnp.sum((expert_id[:, None] == jnp.arange(n_experts, dtype=jnp.int32)[None, :]).astype(jnp.int32),
                     axis=0)
    padded = (counts + rows - 1) // rows * rows
    pad_end = jnp.cumsum(padded)
    pad_start = pad_end - padded
    start = jnp.cumsum(counts) - counts
    order = jnp.argsort(expert_id)
    e_sorted = expert_id[order]
    dest = pad_start[e_sorted] + jnp.arange(n_assign, dtype=jnp.int32) - start[e_sorted]
    n_slots = n_assign + n_experts * rows
    nblk = n_slots // rows
    n_used = (pad_end[-1] // rows).astype(jnp.int32)
    blk_start = jnp.arange(nblk, dtype=jnp.int32) * rows
    block_expert = jnp.minimum(jnp.searchsorted(pad_end, blk_start, side="right"), n_experts - 1)
    last_used = block_expert[jnp.maximum(n_used - 1, 0)]
    block_expert = jnp.where(jnp.arange(nblk) < n_used, block_expert, last_used).astype(jnp.int32)
    slot_of_assign = jnp.zeros((n_assign,), jnp.int32).at[order].set(dest.astype(jnp.int32))
    return order, dest, slot_of_assign, block_expert, n_used.reshape(1), n_slots


def _layer(x, c, w_ada, b_ada, g_pre_mix, g_post_mix, g_pre_ffn, g_post_ffn, w_in, rel_bias,
           g_out_a, g_out_b, w_out, w_group, b_group, w_router, b_router, w_gate, w_up, w_down):
    batch, seq, d = x.shape
    t = batch * seq
    n_heads = rel_bias.shape[0]
    d_a = n_heads * HEAD_DIM
    assert w_in.shape[1] == 6 * d_a and w_out.shape[0] == 2 * d_a
    n_groups = w_group.shape[1]
    n_experts = w_router.shape[1]
    per_group = n_experts // n_groups
    assert n_groups + n_experts <= LANES

    c_pad = jnp.zeros((8, d), F32).at[:batch].set(c)
    mod = _adaln(c_pad, w_ada, b_ada)[:batch]
    mod3 = mod.reshape(batch * 6, 1, d)

    x2 = x.reshape(t, d)
    h = _prenorm(x2, g_pre_mix, mod3, seq, 0, 1)

    scale = HEAD_DIM ** -0.5
    col = np.ones((1, 6 * d_a), np.float32)
    col[:, 0:d_a] = scale
    col[:, 3 * d_a:4 * d_a] = scale
    qkv = _proj(h, w_in.astype(BF16), jnp.asarray(col))

    oa = _attn_a(qkv, rel_bias, g_out_a, batch, seq, n_heads)
    ob = _attn_sb(qkv, g_out_b, batch, seq, n_heads, 3 * n_heads)
    y = _outproj(oa, ob, w_out.astype(BF16).reshape(2 * d_a, d))

    w_rt = jnp.zeros((d, LANES), F32).at[:, :n_groups].set(w_group).at[:, n_groups:n_groups + n_experts].set(w_router)
    b_rt = jnp.zeros((1, LANES), F32).at[0, :n_groups].set(b_group).at[0, n_groups:n_groups + n_experts].set(b_router)
    w_rt_hi = w_rt.astype(BF16)
    w_rt_lo = (w_rt - w_rt_hi.astype(F32)).astype(BF16)
    x1, h2, ids, gates = _router(y, x2, g_post_mix, g_pre_ffn, mod3, seq, 2, 3, 4, w_rt_hi, w_rt_lo, b_rt,
                                 n_groups, per_group)

    rows = min(MOE_BLOCK, t)
    expert_id = ids[:, :2].reshape(-1)
    weight = gates[:, :2].reshape(-1)
    order, dest, slot_of_assign, block_expert, n_used, n_slots = _dispatch(expert_id, n_experts, rows)
    slot_tok = jnp.zeros((n_slots,), jnp.int32).at[dest].set((order // 2).astype(jnp.int32))
    slot_w = jnp.zeros((n_slots,), F32).at[dest].set(weight[order])
    ys = _moe(h2, w_gate, w_up, w_down, block_expert, n_used, slot_tok, slot_w, rows)
    out = _combine(ys, slot_of_assign, x1, g_post_ffn, mod3, seq, 5)
    return out.reshape(batch, seq, d)


def kernel(x, c, w_ada, b_ada, g_pre_mix, g_post_mix, g_pre_ffn, g_post_ffn, w_in, rel_bias, g_out_a,
           g_out_b, w_out, w_group, b_group, w_router, b_router, w_gate, w_up, w_down):
    depth = w_ada.shape[0]
    for l in range(depth):
        x = _layer(x, c, w_ada[l], b_ada[l], g_pre_mix[l], g_post_mix[l], g_pre_ffn[l], g_post_ffn[l],
                   w_in[l], rel_bias[l], g_out_a[l], g_out_b[l], w_out[l], w_group[l], b_group[l],
                   w_router[l], b_router[l], w_gate[l], w_up[l], w_down[l])
    return x
```

```python
import functools

import numpy as np
import jax
import jax.numpy as jnp
from jax import lax
from jax.experimental import pallas as pl
from jax.experimental.pallas import tpu as pltpu

F32 = jnp.float32
BF16 = jnp.bfloat16

HEAD_DIM = 128
CHUNK = 64
LEFT_CHUNKS = 8
EPS = 1e-6
NEG_INF = -1e30
LOG2E = 1.4426950408889634
LANES = 128
VMEM_LIMIT_BYTES = 56 * 1024 * 1024

ATT_A_BLOCK = LEFT_CHUNKS * CHUNK
SB_BLOCK = 256
MOE_BLOCK = 512
MOE_HIDDEN_TILE = 128


def _cparams(sem):
    return pltpu.CompilerParams(dimension_semantics=sem, vmem_limit_bytes=VMEM_LIMIT_BYTES)


def _tile(n, pref):
    t = min(n, pref)
    while n % t:
        t //= 2
    return t


def _adaln_kernel(c_ref, w_ref, b_ref, o_ref):
    c = c_ref[...]
    a = c * jax.nn.sigmoid(c)
    o_ref[...] = jnp.dot(a.astype(BF16), w_ref[...].astype(BF16),
                         preferred_element_type=F32) + b_ref[...]


def _adaln(c_pad, w_ada, b_ada):
    d, n = w_ada.shape
    tn = _tile(n, 512)
    return pl.pallas_call(
        _adaln_kernel,
        out_shape=jax.ShapeDtypeStruct((c_pad.shape[0], n), F32),
        grid=(n // tn,),
        in_specs=[pl.BlockSpec((c_pad.shape[0], d), lambda j: (0, 0)),
                  pl.BlockSpec((d, tn), lambda j: (0, j)),
                  pl.BlockSpec((1, tn), lambda j: (0, j))],
        out_specs=pl.BlockSpec((c_pad.shape[0], tn), lambda j: (0, j)),
        compiler_params=_cparams(("arbitrary",)),
        name="adaln",
    )(c_pad, w_ada, b_ada.reshape(1, n))


def _prenorm_kernel(x_ref, g_ref, sh_ref, sc_ref, o_ref):
    x = x_ref[...]
    y = x * lax.rsqrt(jnp.mean(x * x, axis=-1, keepdims=True) + EPS) * g_ref[...]
    o_ref[...] = (y * (1.0 + sc_ref[0]) + sh_ref[0]).astype(o_ref.dtype)


def _prenorm(x2, g, mod3, seq, shift_idx, scale_idx):
    t, d = x2.shape
    tm = _tile(seq, 256)
    per_b = seq // tm
    return pl.pallas_call(
        _prenorm_kernel,
        out_shape=jax.ShapeDtypeStruct((t, d), BF16),
        grid=(t // tm,),
        in_specs=[pl.BlockSpec((tm, d), lambda i: (i, 0)),
                  pl.BlockSpec((1, d), lambda i: (0, 0)),
                  pl.BlockSpec((1, 1, d), lambda i: ((i // per_b) * 6 + shift_idx, 0, 0)),
                  pl.BlockSpec((1, 1, d), lambda i: ((i // per_b) * 6 + scale_idx, 0, 0))],
        out_specs=pl.BlockSpec((tm, d), lambda i: (i, 0)),
        compiler_params=_cparams(("arbitrary",)),
        name="prenorm",
    )(x2, g.reshape(1, d), mod3, mod3)


def _proj_kernel(a_ref, b_ref, s_ref, o_ref):
    acc = jnp.dot(a_ref[...], b_ref[...], preferred_element_type=F32)
    o_ref[...] = (acc * s_ref[...]).astype(o_ref.dtype)


def _proj(a, w, colscale):
    m, k = a.shape
    n = w.shape[1]
    tm, tn = _tile(m, 1024), _tile(n, 1024)
    return pl.pallas_call(
        _proj_kernel,
        out_shape=jax.ShapeDtypeStruct((m, n), BF16),
        grid=(m // tm, n // tn),
        in_specs=[pl.BlockSpec((tm, k), lambda i, j: (i, 0)),
                  pl.BlockSpec((k, tn), lambda i, j: (0, j)),
                  pl.BlockSpec((1, tn), lambda i, j: (0, j))],
        out_specs=pl.BlockSpec((tm, tn), lambda i, j: (i, j)),
        compiler_params=_cparams(("arbitrary", "arbitrary")),
        name="qkv_proj",
    )(a, w, colscale)


def _outproj_kernel(a1_ref, a2_ref, w1_ref, w2_ref, o_ref):
    o_ref[...] = (jnp.dot(a1_ref[...], w1_ref[...], preferred_element_type=F32)
                  + jnp.dot(a2_ref[...], w2_ref[...], preferred_element_type=F32))


def _outproj(oa, ob, w_out):
    m, ka = oa.shape
    kb = ob.shape[1]
    assert ka == kb
    n = w_out.shape[1]
    tm, tn = _tile(m, 1024), _tile(n, 1024)
    return pl.pallas_call(
        _outproj_kernel,
        out_shape=jax.ShapeDtypeStruct((m, n), F32),
        grid=(m // tm, n // tn),
        in_specs=[pl.BlockSpec((tm, ka), lambda i, j: (i, 0)),
                  pl.BlockSpec((tm, kb), lambda i, j: (i, 0)),
                  pl.BlockSpec((ka, tn), lambda i, j: (0, j)),
                  pl.BlockSpec((kb, tn), lambda i, j: (1, j))],
        out_specs=pl.BlockSpec((tm, tn), lambda i, j: (i, j)),
        compiler_params=_cparams(("arbitrary", "arbitrary")),
        name="out_proj",
    )(oa, ob, w_out, w_out)


def _head_rms(o, g):
    return o * lax.rsqrt(jnp.mean(o * o, axis=-1, keepdims=True) + EPS) * g


def _attn_a_kernel(q_ref, kp_ref, kc_ref, vp_ref, vc_ref, bp_ref, bc_ref, g_ref, o_ref):
    i = pl.program_id(2)
    q = q_ref[...]
    dn = (((1,), (1,)), ((), ()))
    sp = lax.dot_general(q, kp_ref[...], dn, preferred_element_type=F32) + bp_ref[0]
    sp = jnp.where(i > 0, sp, NEG_INF)
    sc = lax.dot_general(q, kc_ref[...], dn, preferred_element_type=F32) + bc_ref[0]
    m = jnp.maximum(jnp.max(sp, axis=-1, keepdims=True), jnp.max(sc, axis=-1, keepdims=True))
    pp = jnp.exp(sp - m)
    pc = jnp.exp(sc - m)
    denom = jnp.sum(pp, axis=-1, keepdims=True) + jnp.sum(pc, axis=-1, keepdims=True)
    o = (jnp.dot(pp.astype(BF16), vp_ref[...], preferred_element_type=F32)
         + jnp.dot(pc.astype(BF16), vc_ref[...], preferred_element_type=F32))
    o = o / denom
    o_ref[...] = _head_rms(o, g_ref[0]).astype(o_ref.dtype)


def _attn_a_bias(rel_bias):
    n_heads, n_rel = rel_bias.shape
    max_rel = (n_rel - 1) // 2
    tq = ATT_A_BLOCK
    assert tq > max_rel
    rb = rel_bias.astype(F32)
    r_ext = jnp.concatenate([jnp.broadcast_to(rb[:, :1], (n_heads, tq - 1 - max_rel)), rb,
                             jnp.broadcast_to(rb[:, -1:], (n_heads, 2 * tq - 1 - max_rel))], axis=1)
    period = 3 * tq
    s = jnp.concatenate([r_ext[:, ::-1], jnp.zeros((n_heads, 1), F32)], axis=1)
    x = jnp.tile(s, (1, tq))[:, :tq * (period - 1)].reshape(n_heads, tq, period - 1)
    toep = x[:, :, tq - 1:3 * tq - 1]
    a = np.arange(tq)[:, None]
    b = np.arange(tq)[None, :]
    ok_prev = (b // CHUNK) >= (a // CHUNK)
    ok_cur = (b // CHUNK) <= (a // CHUNK)
    bp = jnp.where(ok_prev[None], toep[:, :, :tq], NEG_INF)
    bc = jnp.where(ok_cur[None], toep[:, :, tq:], NEG_INF)
    return bp, bc


def _attn_a(qkv, rel_bias, g_out, batch, seq, n_heads):
    tq = ATT_A_BLOCK
    assert seq % tq == 0
    nq = seq // tq
    bp, bc = _attn_a_bias(rel_bias)
    koff, voff = n_heads, 2 * n_heads
    q_spec = pl.BlockSpec((tq, HEAD_DIM), lambda b, h, i: (b * nq + i, h))

    def prev(off):
        return pl.BlockSpec((tq, HEAD_DIM), lambda b, h, i: (b * nq + jnp.maximum(i - 1, 0), off + h))

    def cur(off):
        return pl.BlockSpec((tq, HEAD_DIM), lambda b, h, i: (b * nq + i, off + h))

    bias_spec = pl.BlockSpec((1, tq, tq), lambda b, h, i: (h, 0, 0))
    return pl.pallas_call(
        _attn_a_kernel,
        out_shape=jax.ShapeDtypeStruct((batch * seq, n_heads * HEAD_DIM), BF16),
        grid=(batch, n_heads, nq),
        in_specs=[q_spec, prev(koff), cur(koff), prev(voff), cur(voff), bias_spec, bias_spec,
                  pl.BlockSpec((1, 1, HEAD_DIM), lambda b, h, i: (h, 0, 0))],
        out_specs=pl.BlockSpec((tq, HEAD_DIM), lambda b, h, i: (b * nq + i, h)),
        compiler_params=_cparams(("arbitrary", "arbitrary", "arbitrary")),
        name="attn_chunked",
    )(qkv, qkv, qkv, qkv, qkv, bp, bc, g_out.reshape(n_heads, 1, HEAD_DIM))


def _neg_abs(x):
    bits = pltpu.bitcast(x, jnp.uint32) | jnp.uint32(0x80000000)
    return pltpu.bitcast(bits, F32)


def _sb_tile(q, k, v, tri, acc_ref, car_ref, rows, causal, first):
    n, tk = q.shape[0], k.shape[0]
    z = lax.dot_general(q, k, (((1,), (1,)), ((), ())), preferred_element_type=F32)
    t = jnp.log(1.0 + jnp.exp2(_neg_abs(z))) * LOG2E
    log_beta = jnp.minimum(z, 0.0) - t
    log_keep = log_beta - z
    if causal is not None:
        log_keep = jnp.where(causal, log_keep, 0.0)
    after = jnp.dot(log_keep.astype(BF16), tri, preferred_element_type=F32)
    if not first:
        after = after + jnp.concatenate([car_ref[rows, :]] * (tk // LANES), axis=1)
    a = jnp.exp2(log_beta + after)
    if causal is not None:
        a = jnp.where(causal, a, 0.0)
    pv = jnp.dot(a.astype(BF16), v, preferred_element_type=F32)
    row_sum = jnp.broadcast_to(jnp.sum(log_keep, axis=-1, keepdims=True), (n, LANES))
    if first:
        acc_ref[rows, :] = pv
        car_ref[rows, :] = row_sum
    else:
        acc_ref[rows, :] += pv
        car_ref[rows, :] += row_sum


def _sb_kernel(q_ref, k_ref, v_ref, g_ref, o_ref, acc_ref, car_ref, *, tk):
    qi = pl.program_id(2)
    row = lax.broadcasted_iota(jnp.int32, (tk, tk), 0)
    col = lax.broadcasted_iota(jnp.int32, (tk, tk), 1)
    tri = (row > col).astype(BF16)
    causal = col < row

    def kv(j):
        start = pl.multiple_of(j * tk, tk)
        return k_ref[pl.ds(start, tk), :], v_ref[pl.ds(start, tk), :]

    lo, hi, both = pl.ds(0, tk), pl.ds(tk, tk), pl.ds(0, 2 * tk)
    k1, v1 = kv(2 * qi + 1)
    k0, v0 = kv(2 * qi)
    _sb_tile(q_ref[hi, :], k1, v1, tri, acc_ref, car_ref, hi, causal, True)
    _sb_tile(q_ref[hi, :], k0, v0, tri, acc_ref, car_ref, hi, None, False)
    _sb_tile(q_ref[lo, :], k0, v0, tri, acc_ref, car_ref, lo, causal, True)

    def body(it, c):
        j = 2 * qi - 1 - 2 * it
        ka, va = kv(j)
        kb, vb = kv(j - 1)
        _sb_tile(q_ref[...], ka, va, tri, acc_ref, car_ref, both, None, False)
        _sb_tile(q_ref[...], kb, vb, tri, acc_ref, car_ref, both, None, False)
        return c

    lax.fori_loop(0, qi, body, 0)
    o_ref[...] = _head_rms(acc_ref[...], g_ref[0]).astype(o_ref.dtype)


def _attn_sb(qkv, g_out, batch, seq, n_heads, col0):
    tk = SB_BLOCK
    tq = 2 * tk
    assert seq % tq == 0
    nq = seq // tq
    qoff, koff, voff = col0, col0 + n_heads, col0 + 2 * n_heads
    return pl.pallas_call(
        functools.partial(_sb_kernel, tk=tk),
        out_shape=jax.ShapeDtypeStruct((batch * seq, n_heads * HEAD_DIM), BF16),
        grid=(batch, n_heads, nq),
        in_specs=[pl.BlockSpec((tq, HEAD_DIM), lambda b, h, i: (b * nq + i, qoff + h)),
                  pl.BlockSpec((seq, HEAD_DIM), lambda b, h, i: (b, koff + h)),
                  pl.BlockSpec((seq, HEAD_DIM), lambda b, h, i: (b, voff + h)),
                  pl.BlockSpec((1, 1, HEAD_DIM), lambda b, h, i: (h, 0, 0))],
        out_specs=pl.BlockSpec((tq, HEAD_DIM), lambda b, h, i: (b * nq + i, h)),
        scratch_shapes=[pltpu.VMEM((tq, HEAD_DIM), F32), pltpu.VMEM((tq, LANES), F32)],
        compiler_params=_cparams(("arbitrary", "arbitrary", "arbitrary")),
        name="attn_stickbreak",
    )(qkv, qkv, qkv, g_out.reshape(n_heads, 1, HEAD_DIM))


def _split_bf16(x):
    hi = x.astype(BF16)
    lo = (x - hi.astype(F32)).astype(BF16)
    return hi, lo


def _router_kernel(y_ref, x_ref, gpost_ref, gate_ref, gpre_ref, sh_ref, sc_ref, whi_ref, wlo_ref,
                   brt_ref, x1_ref, h2_ref, ids_ref, gates_ref, *, n_groups, per_group):
    y = y_ref[...]
    yn = y * lax.rsqrt(jnp.mean(y * y, axis=-1, keepdims=True) + EPS) * gpost_ref[...]
    x1 = x_ref[...] + gate_ref[0] * yn
    x1_ref[...] = x1
    hn = x1 * lax.rsqrt(jnp.mean(x1 * x1, axis=-1, keepdims=True) + EPS) * gpre_ref[...]
    h2 = hn * (1.0 + sc_ref[0]) + sh_ref[0]
    h2_ref[...] = h2
    hi, lo = _split_bf16(h2)
    whi = whi_ref[...]
    logits = (jnp.dot(hi, whi, preferred_element_type=F32)
              + jnp.dot(lo, whi, preferred_element_type=F32)
              + jnp.dot(hi, wlo_ref[...], preferred_element_type=F32)) + brt_ref[...]
    lane = lax.broadcasted_iota(jnp.int32, logits.shape, 1)
    big = jnp.int32(LANES)
    is_g = lane < n_groups
    gl = jnp.where(is_g, logits, -jnp.inf)
    gmax = jnp.max(gl, axis=-1, keepdims=True)
    gsel = jnp.min(jnp.where(gl == gmax, lane, big), axis=-1, keepdims=True)
    p_group = 1.0 / jnp.sum(jnp.where(is_g, jnp.exp(logits - gmax), 0.0), axis=-1, keepdims=True)
    lo_lane = n_groups + gsel * per_group
    in_grp = (lane >= lo_lane) & (lane < lo_lane + per_group)
    el = jnp.where(in_grp, logits, -jnp.inf)
    v1 = jnp.max(el, axis=-1, keepdims=True)
    i1 = jnp.min(jnp.where(el == v1, lane, big), axis=-1, keepdims=True)
    el2 = jnp.where(lane == i1, -jnp.inf, el)
    v2 = jnp.max(el2, axis=-1, keepdims=True)
    i2 = jnp.min(jnp.where(el2 == v2, lane, big), axis=-1, keepdims=True)
    e21 = jnp.exp(v2 - v1)
    w1 = p_group / (1.0 + e21)
    w2 = p_group * e21 / (1.0 + e21)
    ids_ref[...] = jnp.where(lane == 0, i1 - n_groups, jnp.where(lane == 1, i2 - n_groups, 0))
    gates_ref[...] = jnp.where(lane == 0, w1, jnp.where(lane == 1, w2, 0.0))


def _router(y, x2, g_post, g_pre, mod3, seq, gate_idx, shift_idx, scale_idx, w_rt_hi, w_rt_lo, b_rt,
            n_groups, per_group):
    t, d = x2.shape
    tm = _tile(seq, 256)
    per_b = seq // tm
    row = pl.BlockSpec((tm, d), lambda i: (i, 0))
    vec = pl.BlockSpec((1, d), lambda i: (0, 0))

    def modspec(idx):
        return pl.BlockSpec((1, 1, d), lambda i: ((i // per_b) * 6 + idx, 0, 0))

    wspec = pl.BlockSpec((d, LANES), lambda i: (0, 0))
    lane_out = pl.BlockSpec((tm, LANES), lambda i: (i, 0))
    return pl.pallas_call(
        functools.partial(_router_kernel, n_groups=n_groups, per_group=per_group),
        out_shape=(jax.ShapeDtypeStruct((t, d), F32), jax.ShapeDtypeStruct((t, d), F32),
                   jax.ShapeDtypeStruct((t, LANES), jnp.int32), jax.ShapeDtypeStruct((t, LANES), F32)),
        grid=(t // tm,),
        in_specs=[row, row, vec, modspec(gate_idx), vec, modspec(shift_idx), modspec(scale_idx),
                  wspec, wspec, pl.BlockSpec((1, LANES), lambda i: (0, 0))],
        out_specs=(row, row, lane_out, lane_out),
        compiler_params=_cparams(("arbitrary",)),
        name="residual_router",
    )(y, x2, g_post.reshape(1, d), mod3, g_pre.reshape(1, d), mod3, mod3, w_rt_hi, w_rt_lo, b_rt)


def _row_copy(src_hbm, dst, src_row, dst_row, sem):
    return pltpu.make_async_copy(src_hbm.at[pl.ds(src_row, 1), :], dst.at[pl.ds(dst_row, 1), :], sem)


def _moe_kernel(be_ref, nused_ref, tok_ref, w_ref, h2_hbm, wg_ref, wu_ref, wd_ref, o_ref,
                xs32, xsb, sem, *, rows, n_hidden_tiles):
    i = pl.program_id(0)
    j = pl.program_id(1)
    valid = i < nused_ref[0]

    @pl.when(jnp.logical_and(valid, j == 0))
    def _():
        def issue(r, c):
            _row_copy(h2_hbm, xs32, tok_ref[0, 0, r], r, sem).start()
            return c

        lax.fori_loop(0, rows, issue, 0)

        def wait(r, c):
            _row_copy(h2_hbm, xs32, 0, r, sem).wait()
            return c

        lax.fori_loop(0, rows, wait, 0)
        xsb[...] = xs32[...].astype(BF16)
        o_ref[...] = jnp.zeros_like(o_ref)

    @pl.when(jnp.logical_and(jnp.logical_not(valid), j == 0))
    def _():
        o_ref[...] = jnp.zeros_like(o_ref)

    @pl.when(valid)
    def _():
        x = xsb[...]
        g = jnp.dot(x, wg_ref[0].astype(BF16), preferred_element_type=F32)
        u = jnp.dot(x, wu_ref[0].astype(BF16), preferred_element_type=F32)
        hmid = (g * jax.nn.sigmoid(g)) * u
        o_ref[...] += jnp.dot(hmid.astype(BF16), wd_ref[0].astype(BF16), preferred_element_type=F32)

    @pl.when(jnp.logical_and(valid, j == n_hidden_tiles - 1))
    def _():
        o_ref[...] = o_ref[...] * w_ref[...]


def _moe(h2, w_gate, w_up, w_down, block_expert, n_used, slot_tok, slot_w, rows):
    t, d = h2.shape
    n_exp, _, dh = w_gate.shape
    n_slots = slot_tok.shape[0]
    nblk = n_slots // rows
    th = _tile(dh, MOE_HIDDEN_TILE)
    nh = dh // th

    def hid(i, j, be, nu):
        return jnp.where(i < nu[0], j, nh - 1)

    grid_spec = pltpu.PrefetchScalarGridSpec(
        num_scalar_prefetch=2,
        grid=(nblk, nh),
        in_specs=[pl.BlockSpec((1, 1, rows), lambda i, j, be, nu: (i, 0, 0), memory_space=pltpu.SMEM),
                  pl.BlockSpec((rows, 1), lambda i, j, be, nu: (i, 0)),
                  pl.BlockSpec(memory_space=pl.ANY),
                  pl.BlockSpec((1, d, th), lambda i, j, be, nu: (be[i], 0, hid(i, j, be, nu))),
                  pl.BlockSpec((1, d, th), lambda i, j, be, nu: (be[i], 0, hid(i, j, be, nu))),
                  pl.BlockSpec((1, th, d), lambda i, j, be, nu: (be[i], hid(i, j, be, nu), 0))],
        out_specs=pl.BlockSpec((rows, d), lambda i, j, be, nu: (i, 0)),
        scratch_shapes=[pltpu.VMEM((rows, d), F32), pltpu.VMEM((rows, d), BF16),
                        pltpu.SemaphoreType.DMA(())],
    )
    return pl.pallas_call(
        functools.partial(_moe_kernel, rows=rows, n_hidden_tiles=nh),
        out_shape=jax.ShapeDtypeStruct((n_slots, d), F32),
        grid_spec=grid_spec,
        compiler_params=_cparams(("arbitrary", "arbitrary")),
        name="moe_experts",
    )(block_expert, n_used, slot_tok.reshape(nblk, 1, rows), slot_w.reshape(n_slots, 1), h2,
      w_gate, w_up, w_down)


def _combine_kernel(slots_ref, ys_hbm, x1_ref, g_ref, gate_ref, o_ref, buf, sem, *, tm):
    def issue(r, c):
        _row_copy(ys_hbm, buf.at[0], slots_ref[0, 0, 2 * r], r, sem).start()
        _row_copy(ys_hbm, buf.at[1], slots_ref[0, 0, 2 * r + 1], r, sem).start()
        return c

    lax.fori_loop(0, tm, issue, 0)

    def wait(r, c):
        _row_copy(ys_hbm, buf.at[0], 0, r, sem).wait()
        _row_copy(ys_hbm, buf.at[1], 0, r, sem).wait()
        return c

    lax.fori_loop(0, tm, wait, 0)
    y = buf[0] + buf[1]
    yn = y * lax.rsqrt(jnp.mean(y * y, axis=-1, keepdims=True) + EPS) * g_ref[...]
    o_ref[...] = x1_ref[...] + gate_ref[0] * yn


def _combine(ys, slots, x1, g_post, mod3, seq, gate_idx):
    t, d = x1.shape
    tm = _tile(seq, 256)
    per_b = seq // tm
    nt = t // tm
    return pl.pallas_call(
        functools.partial(_combine_kernel, tm=tm),
        out_shape=jax.ShapeDtypeStruct((t, d), F32),
        grid=(nt,),
        in_specs=[pl.BlockSpec((1, 1, 2 * tm), lambda i: (i, 0, 0), memory_space=pltpu.SMEM),
                  pl.BlockSpec(memory_space=pl.ANY),
                  pl.BlockSpec((tm, d), lambda i: (i, 0)),
                  pl.BlockSpec((1, d), lambda i: (0, 0)),
                  pl.BlockSpec((1, 1, d), lambda i: ((i // per_b) * 6 + gate_idx, 0, 0))],
        out_specs=pl.BlockSpec((tm, d), lambda i: (i, 0)),
        scratch_shapes=[pltpu.VMEM((2, tm, d), F32), pltpu.SemaphoreType.DMA(())],
        compiler_params=_cparams(("arbitrary",)),
        name="moe_combine",
    )(slots.reshape(nt, 1, 2 * tm), ys, x1, g_post.reshape(1, d), mod3)


def _dispatch(expert_id, n_experts, rows):
    n_assign = expert_id.shape[0]
    counts = jnp.sum((expert_id[:, None] == jnp.arange(n_experts, dtype=jnp.int32)[None, :]).astype(jnp.int32),
                     axis=0)
    padded = (counts + rows - 1) // rows * rows
    pad_end = jnp.cumsum(padded)
    pad_start = pad_end - padded
    start = jnp.cumsum(counts) - counts
    order = jnp.argsort(expert_id)
    e_sorted = expert_id[order]
    dest = pad_start[e_sorted] + jnp.arange(n_assign, dtype=jnp.int32) - start[e_sorted]
    n_slots = n_assign + n_experts * rows
    nblk = n_slots // rows
    n_used = (pad_end[-1] // rows).astype(jnp.int32)
    blk_start = jnp.arange(nblk, dtype=jnp.int32) * rows
    block_expert = jnp.minimum(jnp.searchsorted(pad_end, blk_start, side="right"), n_experts - 1)
    last_used = block_expert[jnp.maximum(n_used - 1, 0)]
    block_expert = jnp.where(jnp.arange(nblk) < n_used, block_expert, last_used).astype(jnp.int32)
    slot_of_assign = jnp.zeros((n_assign,), jnp.int32).at[order].set(dest.astype(jnp.int32))
    return order, dest, slot_of_assign, block_expert, n_used.reshape(1), n_slots


def _layer(x, c, w_ada, b_ada, g_pre_mix, g_post_mix, g_pre_ffn, g_post_ffn, w_in, rel_bias,
           g_out_a, g_out_b, w_out, w_group, b_group, w_router, b_router, w_gate, w_up, w_down):
    batch, seq, d = x.shape
    t = batch * seq
    n_heads = rel_bias.shape[0]
    d_a = n_heads * HEAD_DIM
    assert w_in.shape[1] == 6 * d_a and w_out.shape[0] == 2 * d_a
    n_groups = w_group.shape[1]
    n_experts = w_router.shape[1]
    per_group = n_experts // n_groups
    assert n_groups + n_experts <= LANES

    c_pad = jnp.zeros((8, d), F32).at[:batch].set(c)
    mod = _adaln(c_pad, w_ada, b_ada)[:batch]
    mod3 = mod.reshape(batch * 6, 1, d)

    x2 = x.reshape(t, d)
    h = _prenorm(x2, g_pre_mix, mod3, seq, 0, 1)

    scale = HEAD_DIM ** -0.5
    col = np.ones((1, 6 * d_a), np.float32)
    col[:, 0:d_a] = scale
    col[:, 3 * d_a:4 * d_a] = scale * LOG2E
    qkv = _proj(h, w_in.astype(BF16), jnp.asarray(col))

    oa = _attn_a(qkv, rel_bias, g_out_a, batch, seq, n_heads)
    ob = _attn_sb(qkv, g_out_b, batch, seq, n_heads, 3 * n_heads)
    y = _outproj(oa, ob, w_out.astype(BF16).reshape(2 * d_a, d))

    w_rt = jnp.zeros((d, LANES), F32).at[:, :n_groups].set(w_group).at[:, n_groups:n_groups + n_experts].set(w_router)
    b_rt = jnp.zeros((1, LANES), F32).at[0, :n_groups].set(b_group).at[0, n_groups:n_groups + n_experts].set(b_router)
    w_rt_hi = w_rt.astype(BF16)
    w_rt_lo = (w_rt - w_rt_hi.astype(F32)).astype(BF16)
    x1, h2, ids, gates = _router(y, x2, g_post_mix, g_pre_ffn, mod3, seq, 2, 3, 4, w_rt_hi, w_rt_lo, b_rt,
                                 n_groups, per_group)

    rows = min(MOE_BLOCK, t)
    expert_id = ids[:, :2].reshape(-1)
    weight = gates[:, :2].reshape(-1)
    order, dest, slot_of_assign, block_expert, n_used, n_slots = _dispatch(expert_id, n_experts, rows)
    slot_tok = jnp.zeros((n_slots,), jnp.int32).at[dest].set((order // 2).astype(jnp.int32))
    slot_w = jnp.zeros((n_slots,), F32).at[dest].set(weight[order])
    ys = _moe(h2, w_gate, w_up, w_down, block_expert, n_used, slot_tok, slot_w, rows)
    out = _combine(ys, slot_of_assign, x1, g_post_ffn, mod3, seq, 5)
    return out.reshape(batch, seq, d)


def kernel(x, c, w_ada, b_ada, g_pre_mix, g_post_mix, g_pre_ffn, g_post_ffn, w_in, rel_bias, g_out_a,
           g_out_b, w_out, w_group, b_group, w_router, b_router, w_gate, w_up, w_down):
    depth = w_ada.shape[0]
    for l in range(depth):
        x = _layer(x, c, w_ada[l], b_ada[l], g_pre_mix[l], g_post_mix[l], g_pre_ffn[l], g_post_ffn[l],
                   w_in[l], rel_bias[l], g_out_a[l], g_out_b[l], w_out[l], w_group[l], b_group[l],
                   w_router[l], b_router[l], w_gate[l], w_up[l], w_down[l])
    return x
```

```python
import functools

import numpy as np
import jax
import jax.numpy as jnp
from jax import lax
from jax.experimental import pallas as pl
from jax.experimental.pallas import tpu as pltpu

F32 = jnp.float32
BF16 = jnp.bfloat16

HEAD_DIM = 128
CHUNK = 64
LEFT_CHUNKS = 8
EPS = 1e-6
NEG_INF = -1e30
LOG2E = 1.4426950408889634
LANES = 128
VMEM_LIMIT_BYTES = 56 * 1024 * 1024

ATT_A_BLOCK = LEFT_CHUNKS * CHUNK
SB_BLOCK = 256
SB_MAX_LOG2 = 126.0
MOE_BLOCK = 256
MOE_WEIGHT_CHUNK_ELEMS = 256 * 1024


def _cparams(sem):
    return pltpu.CompilerParams(dimension_semantics=sem, vmem_limit_bytes=VMEM_LIMIT_BYTES)


def _tile(n, pref):
    t = min(n, pref)
    while n % t:
        t //= 2
    return t


def _adaln_kernel(c_ref, w_ref, b_ref, o_ref):
    c = c_ref[...]
    a = c * jax.nn.sigmoid(c)
    o_ref[...] = jnp.dot(a.astype(BF16), w_ref[...].astype(BF16),
                         preferred_element_type=F32) + b_ref[...]


def _adaln(c_pad, w_ada, b_ada):
    d, n = w_ada.shape
    tn = _tile(n, 512)
    return pl.pallas_call(
        _adaln_kernel,
        out_shape=jax.ShapeDtypeStruct((c_pad.shape[0], n), F32),
        grid=(n // tn,),
        in_specs=[pl.BlockSpec((c_pad.shape[0], d), lambda j: (0, 0)),
                  pl.BlockSpec((d, tn), lambda j: (0, j)),
                  pl.BlockSpec((1, tn), lambda j: (0, j))],
        out_specs=pl.BlockSpec((c_pad.shape[0], tn), lambda j: (0, j)),
        compiler_params=_cparams(("arbitrary",)),
        name="adaln",
    )(c_pad, w_ada, b_ada.reshape(1, n))


def _prenorm_kernel(x_ref, g_ref, sh_ref, sc_ref, o_ref):
    x = x_ref[...]
    y = x * lax.rsqrt(jnp.mean(x * x, axis=-1, keepdims=True) + EPS) * g_ref[...]
    o_ref[...] = (y * (1.0 + sc_ref[0]) + sh_ref[0]).astype(o_ref.dtype)


def _prenorm(x2, g, mod3, seq, shift_idx, scale_idx):
    t, d = x2.shape
    tm = _tile(seq, 256)
    per_b = seq // tm
    return pl.pallas_call(
        _prenorm_kernel,
        out_shape=jax.ShapeDtypeStruct((t, d), BF16),
        grid=(t // tm,),
        in_specs=[pl.BlockSpec((tm, d), lambda i: (i, 0)),
                  pl.BlockSpec((1, d), lambda i: (0, 0)),
                  pl.BlockSpec((1, 1, d), lambda i: ((i // per_b) * 6 + shift_idx, 0, 0)),
                  pl.BlockSpec((1, 1, d), lambda i: ((i // per_b) * 6 + scale_idx, 0, 0))],
        out_specs=pl.BlockSpec((tm, d), lambda i: (i, 0)),
        compiler_params=_cparams(("arbitrary",)),
        name="prenorm",
    )(x2, g.reshape(1, d), mod3, mod3)


def _proj_kernel(a_ref, b_ref, s_ref, o_ref):
    acc = jnp.dot(a_ref[...], b_ref[...], preferred_element_type=F32)
    o_ref[...] = (acc * s_ref[...]).astype(o_ref.dtype)


def _proj(a, w, colscale):
    m, k = a.shape
    n = w.shape[1]
    tm, tn = _tile(m, 1024), _tile(n, 1024)
    return pl.pallas_call(
        _proj_kernel,
        out_shape=jax.ShapeDtypeStruct((m, n), BF16),
        grid=(m // tm, n // tn),
        in_specs=[pl.BlockSpec((tm, k), lambda i, j: (i, 0)),
                  pl.BlockSpec((k, tn), lambda i, j: (0, j)),
                  pl.BlockSpec((1, tn), lambda i, j: (0, j))],
        out_specs=pl.BlockSpec((tm, tn), lambda i, j: (i, j)),
        compiler_params=_cparams(("arbitrary", "arbitrary")),
        name="qkv_proj",
    )(a, w, colscale)


def _outproj_kernel(a1_ref, a2_ref, w1_ref, w2_ref, o_ref):
    o_ref[...] = (jnp.dot(a1_ref[...], w1_ref[...], preferred_element_type=F32)
                  + jnp.dot(a2_ref[...], w2_ref[...], preferred_element_type=F32))


def _outproj(oa, ob, w_out):
    m, ka = oa.shape
    kb = ob.shape[1]
    assert ka == kb
    n = w_out.shape[1]
    tm, tn = _tile(m, 1024), _tile(n, 1024)
    return pl.pallas_call(
        _outproj_kernel,
        out_shape=jax.ShapeDtypeStruct((m, n), F32),
        grid=(m // tm, n // tn),
        in_specs=[pl.BlockSpec((tm, ka), lambda i, j: (i, 0)),
                  pl.BlockSpec((tm, kb), lambda i, j: (i, 0)),
                  pl.BlockSpec((ka, tn), lambda i, j: (0, j)),
                  pl.BlockSpec((kb, tn), lambda i, j: (1, j))],
        out_specs=pl.BlockSpec((tm, tn), lambda i, j: (i, j)),
        compiler_params=_cparams(("arbitrary", "arbitrary")),
        name="out_proj",
    )(oa, ob, w_out, w_out)


def _head_rms(o, g):
    return o * lax.rsqrt(jnp.mean(o * o, axis=-1, keepdims=True) + EPS) * g


def _attn_a_kernel(q_ref, kp_ref, kc_ref, vp_ref, vc_ref, bp_ref, bc_ref, g_ref, o_ref):
    i = pl.program_id(2)
    q = q_ref[...]
    dn = (((1,), (1,)), ((), ()))
    sp = lax.dot_general(q, kp_ref[...], dn, preferred_element_type=F32) + bp_ref[0]
    sp = jnp.where(i > 0, sp, NEG_INF)
    sc = lax.dot_general(q, kc_ref[...], dn, preferred_element_type=F32) + bc_ref[0]
    m = jnp.maximum(jnp.max(sp, axis=-1, keepdims=True), jnp.max(sc, axis=-1, keepdims=True))
    pp = jnp.exp(sp - m)
    pc = jnp.exp(sc - m)
    denom = jnp.sum(pp, axis=-1, keepdims=True) + jnp.sum(pc, axis=-1, keepdims=True)
    o = (jnp.dot(pp.astype(BF16), vp_ref[...], preferred_element_type=F32)
         + jnp.dot(pc.astype(BF16), vc_ref[...], preferred_element_type=F32))
    o = o / denom
    o_ref[...] = _head_rms(o, g_ref[0]).astype(o_ref.dtype)


def _attn_a_bias(rel_bias):
    n_heads, n_rel = rel_bias.shape
    max_rel = (n_rel - 1) // 2
    tq = ATT_A_BLOCK
    assert tq > max_rel
    rb = rel_bias.astype(F32)
    r_ext = jnp.concatenate([jnp.broadcast_to(rb[:, :1], (n_heads, tq - 1 - max_rel)), rb,
                             jnp.broadcast_to(rb[:, -1:], (n_heads, 2 * tq - 1 - max_rel))], axis=1)
    period = 3 * tq
    s = jnp.concatenate([r_ext[:, ::-1], jnp.zeros((n_heads, 1), F32)], axis=1)
    x = jnp.tile(s, (1, tq))[:, :tq * (period - 1)].reshape(n_heads, tq, period - 1)
    toep = x[:, :, tq - 1:3 * tq - 1]
    a = np.arange(tq)[:, None]
    b = np.arange(tq)[None, :]
    ok_prev = (b // CHUNK) >= (a // CHUNK)
    ok_cur = (b // CHUNK) <= (a // CHUNK)
    bp = jnp.where(ok_prev[None], toep[:, :, :tq], NEG_INF)
    bc = jnp.where(ok_cur[None], toep[:, :, tq:], NEG_INF)
    return bp, bc


def _attn_a(qkv, rel_bias, g_out, batch, seq, n_heads):
    tq = ATT_A_BLOCK
    assert seq % tq == 0
    nq = seq // tq
    bp, bc = _attn_a_bias(rel_bias)
    koff, voff = n_heads, 2 * n_heads
    q_spec = pl.BlockSpec((tq, HEAD_DIM), lambda b, h, i: (b * nq + i, h))

    def prev(off):
        return pl.BlockSpec((tq, HEAD_DIM), lambda b, h, i: (b * nq + jnp.maximum(i - 1, 0), off + h))

    def cur(off):
        return pl.BlockSpec((tq, HEAD_DIM), lambda b, h, i: (b * nq + i, off + h))

    bias_spec = pl.BlockSpec((1, tq, tq), lambda b, h, i: (h, 0, 0))
    return pl.pallas_call(
        _attn_a_kernel,
        out_shape=jax.ShapeDtypeStruct((batch * seq, n_heads * HEAD_DIM), BF16),
        grid=(batch, n_heads, nq),
        in_specs=[q_spec, prev(koff), cur(koff), prev(voff), cur(voff), bias_spec, bias_spec,
                  pl.BlockSpec((1, 1, HEAD_DIM), lambda b, h, i: (h, 0, 0))],
        out_specs=pl.BlockSpec((tq, HEAD_DIM), lambda b, h, i: (b * nq + i, h)),
        compiler_params=_cparams(("arbitrary", "arbitrary", "arbitrary")),
        name="attn_chunked",
    )(qkv, qkv, qkv, qkv, qkv, bp, bc, g_out.reshape(n_heads, 1, HEAD_DIM))


def _sb_front(q, k, tri, s_ref, slot, car_ref, visible, first):
    n, tk = q.shape[0], k.shape[0]
    z = lax.dot_general(q, k, (((1,), (1,)), ((), ())), preferred_element_type=F32)
    zc = jnp.minimum(z, SB_MAX_LOG2)
    log_keep = jnp.log(1.0 + jnp.exp2(zc)) * (-LOG2E)
    log_beta = zc + log_keep
    if visible is not None:
        log_keep = jnp.where(visible, log_keep, 0.0)
    after = jnp.dot(log_keep.astype(BF16), tri, preferred_element_type=F32)
    if not first:
        after = after + jnp.concatenate([car_ref[...]] * (tk // LANES), axis=1)
    expo = log_beta + after
    if visible is not None:
        expo = jnp.where(visible, expo, NEG_INF)
    s_ref[slot] = expo
    row_sum = jnp.broadcast_to(jnp.sum(log_keep, axis=-1, keepdims=True), (n, LANES))
    if first:
        car_ref[...] = row_sum
    else:
        car_ref[...] += row_sum


def _sb_back(v, s_ref, slot, acc_ref, first):
    pv = jnp.dot(jnp.exp2(s_ref[slot]).astype(BF16), v, preferred_element_type=F32)
    if first:
        acc_ref[...] = pv
    else:
        acc_ref[...] += pv


def _sb_kernel(q_ref, k_ref, v_ref, tri_ref, g_ref, o_ref, acc_ref, car_ref, s_ref, *, tk):
    qi = pl.program_id(2)
    tq = 2 * tk
    row = lax.broadcasted_iota(jnp.int32, (tq, tk), 0)
    col = lax.broadcasted_iota(jnp.int32, (tq, tk), 1)
    tri = tri_ref[...]
    q = q_ref[...]

    def kblk(j):
        return k_ref[pl.ds(pl.multiple_of(j * tk, tk), tk), :]

    def vblk(j):
        return v_ref[pl.ds(pl.multiple_of(j * tk, tk), tk), :]

    j1, j0 = 2 * qi + 1, 2 * qi
    _sb_front(q, kblk(j1), tri, s_ref, 0, car_ref, col + tk < row, True)
    _sb_back(vblk(j1), s_ref, 0, acc_ref, True)
    _sb_front(q, kblk(j0), tri, s_ref, 1, car_ref, col < row, False)

    def body(m, c):
        j = j0 - 2 * m
        _sb_back(vblk(j), s_ref, 1, acc_ref, False)
        _sb_front(q, kblk(j - 1), tri, s_ref, 0, car_ref, None, False)
        _sb_back(vblk(j - 1), s_ref, 0, acc_ref, False)
        _sb_front(q, kblk(j - 2), tri, s_ref, 1, car_ref, None, False)
        return c

    lax.fori_loop(0, qi, body, 0)
    _sb_back(vblk(0), s_ref, 1, acc_ref, False)
    o_ref[...] = _head_rms(acc_ref[...], g_ref[0]).astype(o_ref.dtype)


def _attn_sb(qkv, g_out, batch, seq, n_heads, col0):
    tk = SB_BLOCK
    tq = 2 * tk
    assert seq % tq == 0
    nq = seq // tq
    qoff, koff, voff = col0, col0 + n_heads, col0 + 2 * n_heads
    tri = jnp.asarray(np.tril(np.ones((tk, tk), np.float32), -1), BF16)
    return pl.pallas_call(
        functools.partial(_sb_kernel, tk=tk),
        out_shape=jax.ShapeDtypeStruct((batch * seq, n_heads * HEAD_DIM), BF16),
        grid=(batch, n_heads, nq),
        in_specs=[pl.BlockSpec((tq, HEAD_DIM), lambda b, h, i: (b * nq + i, qoff + h)),
                  pl.BlockSpec((seq, HEAD_DIM), lambda b, h, i: (b, koff + h)),
                  pl.BlockSpec((seq, HEAD_DIM), lambda b, h, i: (b, voff + h)),
                  pl.BlockSpec((tk, tk), lambda b, h, i: (0, 0)),
                  pl.BlockSpec((1, 1, HEAD_DIM), lambda b, h, i: (h, 0, 0))],
        out_specs=pl.BlockSpec((tq, HEAD_DIM), lambda b, h, i: (b * nq + i, h)),
        scratch_shapes=[pltpu.VMEM((tq, HEAD_DIM), F32), pltpu.VMEM((tq, LANES), F32),
                        pltpu.VMEM((2, tq, tk), F32)],
        compiler_params=_cparams(("arbitrary", "arbitrary", "arbitrary")),
        name="attn_stickbreak",
    )(qkv, qkv, qkv, tri, g_out.reshape(n_heads, 1, HEAD_DIM))


def _split_bf16(x):
    hi = x.astype(BF16)
    lo = (x - hi.astype(F32)).astype(BF16)
    return hi, lo


def _router_kernel(y_ref, x_ref, gpost_ref, gate_ref, gpre_ref, sh_ref, sc_ref, whi_ref, wlo_ref,
                   brt_ref, tri_ref, x1_ref, h2_ref, ids_ref, gates_ref, cnt_ref, cnt_scr,
                   *, n_groups, per_group):
    @pl.when(pl.program_id(0) == 0)
    def _():
        cnt_scr[...] = jnp.zeros_like(cnt_scr)

    y = y_ref[...]
    yn = y * lax.rsqrt(jnp.mean(y * y, axis=-1, keepdims=True) + EPS) * gpost_ref[...]
    x1 = x_ref[...] + gate_ref[0] * yn
    x1_ref[...] = x1
    hn = x1 * lax.rsqrt(jnp.mean(x1 * x1, axis=-1, keepdims=True) + EPS) * gpre_ref[...]
    h2 = hn * (1.0 + sc_ref[0]) + sh_ref[0]
    h2_ref[...] = h2
    hi, lo = _split_bf16(h2)
    whi = whi_ref[...]
    logits = (jnp.dot(hi, whi, preferred_element_type=F32)
              + jnp.dot(lo, whi, preferred_element_type=F32)
              + jnp.dot(hi, wlo_ref[...], preferred_element_type=F32)) + brt_ref[...]
    lane = lax.broadcasted_iota(jnp.int32, logits.shape, 1)
    big = jnp.int32(LANES)
    is_g = lane < n_groups
    gl = jnp.where(is_g, logits, -jnp.inf)
    gmax = jnp.max(gl, axis=-1, keepdims=True)
    gsel = jnp.min(jnp.where(gl == gmax, lane, big), axis=-1, keepdims=True)
    p_group = 1.0 / jnp.sum(jnp.where(is_g, jnp.exp(logits - gmax), 0.0), axis=-1, keepdims=True)
    lo_lane = n_groups + gsel * per_group
    in_grp = (lane >= lo_lane) & (lane < lo_lane + per_group)
    el = jnp.where(in_grp, logits, -jnp.inf)
    v1 = jnp.max(el, axis=-1, keepdims=True)
    i1 = jnp.min(jnp.where(el == v1, lane, big), axis=-1, keepdims=True)
    el2 = jnp.where(lane == i1, -jnp.inf, el)
    v2 = jnp.max(el2, axis=-1, keepdims=True)
    i2 = jnp.min(jnp.where(el2 == v2, lane, big), axis=-1, keepdims=True)
    e21 = jnp.exp(v2 - v1)
    w1 = p_group / (1.0 + e21)
    w2 = p_group * e21 / (1.0 + e21)
    gates_ref[...] = jnp.where(lane == 0, w1, jnp.where(lane == 1, w2, 0.0))
    hit1 = lane == i1
    hit2 = lane == i2
    onehot = jnp.where(jnp.logical_or(hit1, hit2), 1.0, 0.0)
    before = jnp.dot(tri_ref[...], onehot.astype(BF16), preferred_element_type=F32) + cnt_scr[0:1, :]
    r1 = jnp.sum(jnp.where(hit1, before, 0.0), axis=-1, keepdims=True).astype(jnp.int32)
    r2 = jnp.sum(jnp.where(hit2, before, 0.0), axis=-1, keepdims=True).astype(jnp.int32)
    cnt_scr[0:1, :] += jnp.sum(onehot, axis=0, keepdims=True)
    cnt_ref[...] = jnp.broadcast_to(cnt_scr[0:1, :], cnt_ref.shape).astype(jnp.int32)
    ids_ref[...] = jnp.where(lane == 0, i1 - n_groups,
                             jnp.where(lane == 1, i2 - n_groups,
                                       jnp.where(lane == 2, r1, jnp.where(lane == 3, r2, 0))))


def _router(y, x2, g_post, g_pre, mod3, seq, gate_idx, shift_idx, scale_idx, w_rt_hi, w_rt_lo, b_rt,
            n_groups, per_group):
    t, d = x2.shape
    tm = _tile(seq, 256)
    per_b = seq // tm
    row = pl.BlockSpec((tm, d), lambda i: (i, 0))
    vec = pl.BlockSpec((1, d), lambda i: (0, 0))

    def modspec(idx):
        return pl.BlockSpec((1, 1, d), lambda i: ((i // per_b) * 6 + idx, 0, 0))

    wspec = pl.BlockSpec((d, LANES), lambda i: (0, 0))
    lane_out = pl.BlockSpec((tm, LANES), lambda i: (i, 0))
    tri = jnp.asarray(np.tril(np.ones((tm, tm), np.float32), -1), BF16)
    return pl.pallas_call(
        functools.partial(_router_kernel, n_groups=n_groups, per_group=per_group),
        out_shape=(jax.ShapeDtypeStruct((t, d), F32), jax.ShapeDtypeStruct((t, d), F32),
                   jax.ShapeDtypeStruct((t, LANES), jnp.int32), jax.ShapeDtypeStruct((t, LANES), F32),
                   jax.ShapeDtypeStruct((8, LANES), jnp.int32)),
        grid=(t // tm,),
        in_specs=[row, row, vec, modspec(gate_idx), vec, modspec(shift_idx), modspec(scale_idx),
                  wspec, wspec, pl.BlockSpec((1, LANES), lambda i: (0, 0)),
                  pl.BlockSpec((tm, tm), lambda i: (0, 0))],
        out_specs=(row, row, lane_out, lane_out, pl.BlockSpec((8, LANES), lambda i: (0, 0))),
        scratch_shapes=[pltpu.VMEM((8, LANES), F32)],
        compiler_params=_cparams(("arbitrary",)),
        name="residual_router",
    )(y, x2, g_post.reshape(1, d), mod3, g_pre.reshape(1, d), mod3, mod3, w_rt_hi, w_rt_lo, b_rt, tri)


def _row_copy(src_hbm, dst, src_row, dst_row, sem):
    return pltpu.make_async_copy(src_hbm.at[pl.ds(src_row, 1), :], dst.at[pl.ds(dst_row, 1), :], sem)


def _stream_cast(src_at, dst_store, stage, sem, n_chunks):
    def copy(k, s):
        return pltpu.make_async_copy(src_at(k), stage.at[s], sem.at[s])

    copy(0, 0).start()

    def body(k, c):
        s = lax.rem(k, 2)

        @pl.when(k + 1 < n_chunks)
        def _():
            copy(k + 1, 1 - s).start()

        copy(k, s).wait()
        dst_store(k, stage[s].astype(BF16))
        return c

    lax.fori_loop(0, n_chunks, body, 0)


def _moe_kernel(be_ref, newe_ref, nused_ref, tok_ref, tokn_ref, h2_hbm, wg_hbm, wu_hbm, wd_hbm,
                o_ref, xs32, wgu_b, wd_b, st_a, st_d, sem_x, sem_w, *, rows, d, dh, ca, cd):
    i = pl.program_id(0)
    n_used = nused_ref[0]
    valid = i < n_used
    slot = lax.rem(i, 2)

    def issue_rows(tref, s):
        def body(r, c):
            _row_copy(h2_hbm, xs32.at[s], tref[0, 0, r], r, sem_x.at[s]).start()
            return c

        lax.fori_loop(0, rows, body, 0)

    @pl.when(i == 0)
    def _():
        issue_rows(tok_ref, 0)

    @pl.when(i + 1 < n_used)
    def _():
        issue_rows(tokn_ref, 1 - slot)

    @pl.when(jnp.logical_and(valid, newe_ref[i] == 1))
    def _():
        e = be_ref[i]

        def put_gate(k, val):
            wgu_b[pl.ds(pl.multiple_of(k * ca, ca), ca), pl.ds(0, dh)] = val

        def put_up(k, val):
            wgu_b[pl.ds(pl.multiple_of(k * ca, ca), ca), pl.ds(dh, dh)] = val

        def put_down(k, val):
            wd_b[pl.ds(pl.multiple_of(k * cd, cd), cd), :] = val

        _stream_cast(lambda k: wg_hbm.at[e, pl.ds(k * ca, ca), :], put_gate, st_a, sem_w, d // ca)
        _stream_cast(lambda k: wu_hbm.at[e, pl.ds(k * ca, ca), :], put_up, st_a, sem_w, d // ca)
        _stream_cast(lambda k: wd_hbm.at[e, pl.ds(k * cd, cd), :], put_down, st_d, sem_w, dh // cd)

    @pl.when(valid)
    def _():
        def wait(r, c):
            _row_copy(h2_hbm, xs32.at[slot], 0, r, sem_x.at[slot]).wait()
            return c

        lax.fori_loop(0, rows, wait, 0)
        x = xs32[slot].astype(BF16)
        gu = jnp.dot(x, wgu_b[...], preferred_element_type=F32)
        g = gu[:, :dh]
        u = gu[:, dh:]
        hmid = (g * jax.nn.sigmoid(g)) * u
        o_ref[...] = jnp.dot(hmid.astype(BF16), wd_b[...], preferred_element_type=F32)

    @pl.when(jnp.logical_not(valid))
    def _():
        o_ref[...] = jnp.zeros_like(o_ref)


def _moe(h2, w_gate, w_up, w_down, block_expert, new_expert, n_used, slot_tok, rows):
    t, d = h2.shape
    n_exp, _, dh = w_gate.shape
    n_slots = slot_tok.shape[0]
    nblk = n_slots // rows
    ca = min(d, max(8, MOE_WEIGHT_CHUNK_ELEMS // dh))
    cd = min(dh, max(8, MOE_WEIGHT_CHUNK_ELEMS // d))
    assert d % ca == 0 and dh % cd == 0
    tok3 = slot_tok.reshape(nblk, 1, rows)
    smem_blk = functools.partial(pl.BlockSpec, (1, 1, rows), memory_space=pltpu.SMEM)
    grid_spec = pltpu.PrefetchScalarGridSpec(
        num_scalar_prefetch=3,
        grid=(nblk,),
        in_specs=[smem_blk(lambda i, *_: (i, 0, 0)),
                  smem_blk(lambda i, *_: (jnp.minimum(i + 1, nblk - 1), 0, 0)),
                  pl.BlockSpec(memory_space=pl.ANY), pl.BlockSpec(memory_space=pl.ANY),
                  pl.BlockSpec(memory_space=pl.ANY), pl.BlockSpec(memory_space=pl.ANY)],
        out_specs=pl.BlockSpec((rows, d), lambda i, *_: (i, 0)),
        scratch_shapes=[pltpu.VMEM((2, rows, d), F32),
                        pltpu.VMEM((d, 2 * dh), BF16), pltpu.VMEM((dh, d), BF16),
                        pltpu.VMEM((2, ca, dh), F32), pltpu.VMEM((2, cd, d), F32),
                        pltpu.SemaphoreType.DMA((2,)), pltpu.SemaphoreType.DMA((2,))],
    )
    return pl.pallas_call(
        functools.partial(_moe_kernel, rows=rows, d=d, dh=dh, ca=ca, cd=cd),
        out_shape=jax.ShapeDtypeStruct((n_slots, d), F32),
        grid_spec=grid_spec,
        compiler_params=_cparams(("arbitrary",)),
        name="moe_experts",
    )(block_expert, new_expert, n_used, tok3, tok3, h2, w_gate, w_up, w_down)


def _combine_kernel(slots_ref, ys_hbm, w_ref, x1_ref, g_ref, gate_ref, o_ref, buf, sem, *, tm):
    def issue(r, c):
        _row_copy(ys_hbm, buf.at[0], slots_ref[0, 0, 2 * r], r, sem).start()
        _row_copy(ys_hbm, buf.at[1], slots_ref[0, 0, 2 * r + 1], r, sem).start()
        return c

    lax.fori_loop(0, tm, issue, 0)

    def wait(r, c):
        _row_copy(ys_hbm, buf.at[0], 0, r, sem).wait()
        _row_copy(ys_hbm, buf.at[1], 0, r, sem).wait()
        return c

    lax.fori_loop(0, tm, wait, 0)
    w = w_ref[...]
    y = buf[0] * w[:, 0:1] + buf[1] * w[:, 1:2]
    yn = y * lax.rsqrt(jnp.mean(y * y, axis=-1, keepdims=True) + EPS) * g_ref[...]
    o_ref[...] = x1_ref[...] + gate_ref[0] * yn


def _combine(ys, slots, gates, x1, g_post, mod3, seq, gate_idx):
    t, d = x1.shape
    tm = _tile(seq, 256)
    per_b = seq // tm
    nt = t // tm
    return pl.pallas_call(
        functools.partial(_combine_kernel, tm=tm),
        out_shape=jax.ShapeDtypeStruct((t, d), F32),
        grid=(nt,),
        in_specs=[pl.BlockSpec((1, 1, 2 * tm), lambda i: (i, 0, 0), memory_space=pltpu.SMEM),
                  pl.BlockSpec(memory_space=pl.ANY),
                  pl.BlockSpec((tm, LANES), lambda i: (i, 0)),
                  pl.BlockSpec((tm, d), lambda i: (i, 0)),
                  pl.BlockSpec((1, d), lambda i: (0, 0)),
                  pl.BlockSpec((1, 1, d), lambda i: ((i // per_b) * 6 + gate_idx, 0, 0))],
        out_specs=pl.BlockSpec((tm, d), lambda i: (i, 0)),
        scratch_shapes=[pltpu.VMEM((2, tm, d), F32), pltpu.SemaphoreType.DMA(())],
        compiler_params=_cparams(("arbitrary",)),
        name="moe_combine",
    )(slots.reshape(nt, 1, 2 * tm), ys, gates, x1, g_post.reshape(1, d), mod3)


def _dispatch(expert_id, rank, counts, rows):
    t, k = expert_id.shape
    n_experts = counts.shape[0]
    padded = (counts + rows - 1) // rows * rows
    pad_end = jnp.cumsum(padded)
    pad_start = pad_end - padded
    onehot = expert_id[..., None] == jnp.arange(n_experts, dtype=jnp.int32)
    slots = jnp.sum(jnp.where(onehot, pad_start, 0), axis=-1).astype(jnp.int32) + rank
    n_slots = t * k + n_experts * rows
    nblk = n_slots // rows
    n_used = (pad_end[-1] // rows).astype(jnp.int32)
    blk_start = jnp.arange(nblk, dtype=jnp.int32) * rows
    block_expert = jnp.minimum(jnp.searchsorted(pad_end, blk_start, side="right"), n_experts - 1)
    last_used = block_expert[jnp.maximum(n_used - 1, 0)]
    block_expert = jnp.where(jnp.arange(nblk) < n_used, block_expert, last_used).astype(jnp.int32)
    new_expert = jnp.concatenate([jnp.ones((1,), jnp.int32),
                                  (block_expert[1:] != block_expert[:-1]).astype(jnp.int32)])
    token = jnp.broadcast_to(jnp.arange(t, dtype=jnp.int32)[:, None], (t, k))
    slot_tok = jnp.zeros((n_slots,), jnp.int32).at[slots.reshape(-1)].set(token.reshape(-1))
    return slots, slot_tok, block_expert, new_expert, n_used.reshape(1)


def _layer(x, c, w_ada, b_ada, g_pre_mix, g_post_mix, g_pre_ffn, g_post_ffn, w_in, rel_bias,
           g_out_a, g_out_b, w_out, w_group, b_group, w_router, b_router, w_gate, w_up, w_down):
    batch, seq, d = x.shape
    t = batch * seq
    n_heads = rel_bias.shape[0]
    d_a = n_heads * HEAD_DIM
    assert w_in.shape[1] == 6 * d_a and w_out.shape[0] == 2 * d_a
    n_groups = w_group.shape[1]
    n_experts = w_router.shape[1]
    per_group = n_experts // n_groups
    assert n_groups + n_experts <= LANES

    c_pad = jnp.zeros((8, d), F32).at[:batch].set(c)
    mod = _adaln(c_pad, w_ada, b_ada)[:batch]
    mod3 = mod.reshape(batch * 6, 1, d)

    x2 = x.reshape(t, d)
    h = _prenorm(x2, g_pre_mix, mod3, seq, 0, 1)

    scale = HEAD_DIM ** -0.5
    col = np.ones((1, 6 * d_a), np.float32)
    col[:, 0:d_a] = scale
    col[:, 3 * d_a:4 * d_a] = scale * LOG2E
    qkv = _proj(h, w_in.astype(BF16), jnp.asarray(col))

    oa = _attn_a(qkv, rel_bias, g_out_a, batch, seq, n_heads)
    ob = _attn_sb(qkv, g_out_b, batch, seq, n_heads, 3 * n_heads)
    y = _outproj(oa, ob, w_out.astype(BF16).reshape(2 * d_a, d))

    w_rt = jnp.zeros((d, LANES), F32).at[:, :n_groups].set(w_group).at[:, n_groups:n_groups + n_experts].set(w_router)
    b_rt = jnp.zeros((1, LANES), F32).at[0, :n_groups].set(b_group).at[0, n_groups:n_groups + n_experts].set(b_router)
    w_rt_hi = w_rt.astype(BF16)
    w_rt_lo = (w_rt - w_rt_hi.astype(F32)).astype(BF16)
    x1, h2, ids, gates, cnt = _router(y, x2, g_post_mix, g_pre_ffn, mod3, seq, 2, 3, 4, w_rt_hi, w_rt_lo,
                                      b_rt, n_groups, per_group)

    rows = min(MOE_BLOCK, t)
    counts = cnt[0, n_groups:n_groups + n_experts]
    slots, slot_tok, block_expert, new_expert, n_used = _dispatch(ids[:, 0:2], ids[:, 2:4], counts, rows)
    ys = _moe(h2, w_gate, w_up, w_down, block_expert, new_expert, n_used, slot_tok, rows)
    out = _combine(ys, slots, gates, x1, g_post_ffn, mod3, seq, 5)
    return out.reshape(batch, seq, d)


def kernel(x, c, w_ada, b_ada, g_pre_mix, g_post_mix, g_pre_ffn, g_post_ffn, w_in, rel_bias, g_out_a,
           g_out_b, w_out, w_group, b_group, w_router, b_router, w_gate, w_up, w_down):
    depth = w_ada.shape[0]
    for l in range(depth):
        x = _layer(x, c, w_ada[l], b_ada[l], g_pre_mix[l], g_post_mix[l], g_pre_ffn[l], g_post_ffn[l],
                   w_in[l], rel_bias[l], g_out_a[l], g_out_b[l], w_out[l], w_group[l], b_group[l],
                   w_router[l], b_router[l], w_gate[l], w_up[l], w_down[l])
    return x
```
